```python
import math
import jax, jax.numpy as jnp
from jax import lax
import numpy as np

D_MODEL = 1024
BATCH = 16
SEQ = 2048
DEPTH = 1
DEC_BATCH = 8
DEC_SEQ = 8192
PAST_LEN = 128

HEAD_DIM = 64
A_HEADS = 8
B_Q_HEADS = 8
B_KV_HEADS = 2
B_GROUP = B_Q_HEADS // B_KV_HEADS
A_WIDTH = A_HEADS * HEAD_DIM
B_WIDTH = B_Q_HEADS * HEAD_DIM
KV_B_WIDTH = B_KV_HEADS * HEAD_DIM
MIX_WIDTH = A_WIDTH + B_WIDTH
IN_WIDTH = 3 * A_WIDTH + B_WIDTH + 2 * KV_B_WIDTH
DILATION_PAIRS = ((128, 1), (512, 4), (2048, 16))
B_RADIUS = 128
ROT_DIM = HEAD_DIM // 4
ROPE_THETA = 500000.0
N_KEYS = 128
N_EXPERTS = N_KEYS * N_KEYS
PEER_HEADS = 8
PEER_TOPK = 16
KEY_HALF = 128
QUERY_DIM = 2 * KEY_HALF
PEER_CHUNK = 128
LN_EPS = 1e-5
RMS_EPS = 1e-6
DN_ALPHA = (2 * DEPTH) ** 0.25
DN_BETA = (8 * DEPTH) ** -0.25

kernel_name = 'hymba_dilated_swa_peer_encoder'


def _layer_norm(x, g, b):
    xf = x.astype(jnp.float32)
    mu = jnp.mean(xf, axis=-1, keepdims=True)
    var = jnp.mean(jnp.square(xf - mu), axis=-1, keepdims=True)
    y = (xf - mu) * lax.rsqrt(var + LN_EPS) * g.astype(jnp.float32) + b.astype(jnp.float32)
    return y.astype(x.dtype)


def _rms_norm(x, g):
    xf = x.astype(jnp.float32)
    y = xf * lax.rsqrt(jnp.mean(jnp.square(xf), axis=-1, keepdims=True) + RMS_EPS)
    return y * g.astype(jnp.float32)


def _rope_partial(t, pos):
    half = ROT_DIM // 2
    inv = ROPE_THETA ** (-jnp.arange(half, dtype=jnp.float32) * 2.0 / ROT_DIM)
    ang = pos.astype(jnp.float32)[:, None] * inv[None, :]
    cos = jnp.cos(ang)[None, :, None, :]
    sin = jnp.sin(ang)[None, :, None, :]
    t1 = t[..., :half].astype(jnp.float32)
    t2 = t[..., half:ROT_DIM].astype(jnp.float32)
    rot = jnp.concatenate([t1 * cos - t2 * sin, t2 * cos + t1 * sin], axis=-1).astype(t.dtype)
    return jnp.concatenate([rot, t[..., ROT_DIM:]], axis=-1)


def _banded_attention(q, k, v, radius, block, sink=None):
    n, L, hk, g, dh = q.shape
    nb = -(-L // block)
    lp = nb * block
    qp = jnp.pad(q, ((0, 0), (0, lp - L), (0, 0), (0, 0), (0, 0)))
    kp = jnp.pad(k, ((0, 0), (block, lp - L + block), (0, 0), (0, 0)))
    vp = jnp.pad(v, ((0, 0), (block, lp - L + block), (0, 0), (0, 0)))
    qb = qp.reshape(n, nb, block, hk, g, dh)

    def key_blocks(t):
        return jnp.concatenate(
            [t[:, j * block:j * block + lp].reshape(n, nb, block, hk, dh) for j in range(3)], axis=2)

    kb = key_blocks(kp)
    vb = key_blocks(vp)
    starts = jnp.arange(nb)[:, None] * block
    qpos = starts + jnp.arange(block)[None, :]
    kpos = starts + jnp.arange(3 * block)[None, :] - block
    valid = (jnp.abs(kpos[:, None, :] - qpos[:, :, None]) <= radius) \
        & (kpos >= 0)[:, None, :] & (kpos < L)[:, None, :]
    s = jnp.einsum('nbqhgd,nbkhd->nbhgqk', qb, kb).astype(jnp.float32) * (dh ** -0.5)
    s = jnp.where(valid[None, :, None, None], s, -jnp.inf)
    m = jnp.max(s, axis=-1)
    if sink is not None:
        sk = sink.astype(jnp.float32)[None, None, :, :, None]
        m = jnp.maximum(m, sk)
    p = jnp.exp(s - m[..., None])
    den = jnp.sum(p, axis=-1)
    if sink is not None:
        den = den + jnp.exp(sk - m)
    o = jnp.einsum('nbhgqk,nbkhd->nbqhgd', (p / den[..., None]).astype(v.dtype), vb)
    o = o.reshape(n, lp, hk, g, dh)[:, :L]
    lse = (m + jnp.log(den)).transpose(0, 1, 4, 2, 3).reshape(n, lp, hk, g)[:, :L]
    return o, lse


def _dilated_branch(q, k, v, window, dilation):
    b, s, h, dh = q.shape
    L = s // dilation
    radius = window // (2 * dilation)

    def split(t):
        return t.reshape(b, L, dilation, h, dh).transpose(0, 2, 1, 3, 4).reshape(b * dilation, L, h, dh)

    o, lse = _banded_attention(split(q)[:, :, :, None], split(k), split(v), radius, radius)
    o = o[:, :, :, 0].reshape(b, dilation, L, h, dh).transpose(0, 2, 1, 3, 4).reshape(b, s, h, dh)
    lse = lse[..., 0].reshape(b, dilation, L, h).transpose(0, 2, 1, 3).reshape(b, s, h)
    return o, lse


def _token_mixer(h, w_in, attn_sink, gn_a, gn_b, w_out):
    b, s, _ = h.shape
    pos = jnp.arange(s)
    proj = h @ w_in
    cuts = [A_WIDTH, 2 * A_WIDTH, 3 * A_WIDTH, 3 * A_WIDTH + B_WIDTH, 3 * A_WIDTH + B_WIDTH + KV_B_WIDTH]
    qa, ka, va, qb, kb, vb = jnp.split(proj, cuts, axis=-1)
    qa = _rope_partial(qa.reshape(b, s, A_HEADS, HEAD_DIM), pos)
    ka = _rope_partial(ka.reshape(b, s, A_HEADS, HEAD_DIM), pos)
    va = va.reshape(b, s, A_HEADS, HEAD_DIM)
    qb = _rope_partial(qb.reshape(b, s, B_Q_HEADS, HEAD_DIM), pos)
    kb = _rope_partial(kb.reshape(b, s, B_KV_HEADS, HEAD_DIM), pos)
    vb = vb.reshape(b, s, B_KV_HEADS, HEAD_DIM)

    outs, lses = [], []
    for window, dilation in DILATION_PAIRS:
        o, l = _dilated_branch(qa, ka, va, window, dilation)
        outs.append(o.astype(jnp.float32))
        lses.append(l)
    wts = jax.nn.softmax(jnp.stack(lses), axis=0)
    oa = jnp.einsum('rbsh,rbshd->bshd', wts, jnp.stack(outs)).reshape(b, s, A_WIDTH)

    ob, _ = _banded_attention(qb.reshape(b, s, B_KV_HEADS, B_GROUP, HEAD_DIM), kb, vb,
                              B_RADIUS, B_RADIUS, attn_sink.reshape(B_KV_HEADS, B_GROUP))
    ob = ob.reshape(b, s, B_WIDTH)

    merged = jnp.concatenate([_rms_norm(oa, gn_a), _rms_norm(ob, gn_b)], axis=-1).astype(h.dtype)
    return merged @ w_out


def _peer(h, w_query, sub_keys, expert_down, expert_up):
    b, s, d = h.shape
    chunks = h.reshape(b * s // PEER_CHUNK, PEER_CHUNK, d)
    keys = sub_keys.astype(jnp.float32)

    def retrieve(hc):
        t = hc.shape[0]
        q = (hc @ w_query).reshape(t, PEER_HEADS, 2, KEY_HALF).astype(jnp.float32)
        sh = jnp.einsum('thpd,hpkd->thpk', q, keys)
        top_s, top_i = lax.top_k(sh, PEER_TOPK)
        cand = (top_s[:, :, 0, :, None] + top_s[:, :, 1, None, :]).reshape(t, PEER_HEADS, PEER_TOPK * PEER_TOPK)
        best_s, best = lax.top_k(cand, PEER_TOPK)
        i1 = jnp.take_along_axis(top_i[:, :, 0], best // PEER_TOPK, axis=-1)
        i2 = jnp.take_along_axis(top_i[:, :, 1], best % PEER_TOPK, axis=-1)
        idx = i1 * N_KEYS + i2
        gate = jax.nn.softmax(best_s, axis=-1)
        u = expert_down[idx]
        act = jax.nn.gelu(jnp.einsum('thkd,td->thk', u, hc).astype(jnp.float32), approximate=False)
        wgt = (gate * act).astype(hc.dtype)
        return jnp.einsum('thk,thkd->td', wgt, expert_up[idx])

    return lax.map(retrieve, chunks).reshape(b, s, d)


def _encoder_layer(x, c, w_mod, b_mod, w_in, attn_sink, gn_a, gn_b, w_out, ln1_g, ln1_b,
                   w_query, sub_keys, expert_down, expert_up, ln2_g, ln2_b):
    mod = (jax.nn.silu(c) @ w_mod + b_mod)[:, None, :]
    sh1, sc1, gt1, sh2, sc2, gt2 = jnp.split(mod, 6, axis=-1)
    h = x * (1 + sc1) + sh1
    x = _layer_norm(DN_ALPHA * x + (1 + gt1) * _token_mixer(h, w_in, attn_sink, gn_a, gn_b, w_out), ln1_g, ln1_b)
    h = x * (1 + sc2) + sh2
    x = _layer_norm(DN_ALPHA * x + (1 + gt2) * _peer(h, w_query, sub_keys, expert_down, expert_up), ln2_g, ln2_b)
    return x


def setup_inputs(seed: int = 0) -> dict:
    key = jax.random.key(seed)
    ks = jax.random.split(key, 20)

    def nrm(k, shape, scale):
        return jax.random.normal(k, shape, jnp.float32) * scale

    D = D_MODEL
    return {
        'x_prompt': nrm(ks[0], (BATCH, SEQ, D), 1.0),
        'x_sample': nrm(ks[1], (DEC_BATCH, DEC_SEQ, D), 1.0),
        'c_prompt': nrm(ks[2], (BATCH, D), 1.0),
        'c_sample': nrm(ks[3], (DEC_BATCH, D), 1.0),
        'w_mod': nrm(ks[4], (DEPTH, D, 6 * D), 0.5 * D ** -0.5),
        'b_mod': nrm(ks[5], (DEPTH, 6 * D), 0.01),
        'w_in': nrm(ks[6], (DEPTH, D, IN_WIDTH), D ** -0.5),
        'attn_sink': nrm(ks[7], (DEPTH, B_Q_HEADS), 1.0),
        'gn_a': 1.0 + nrm(ks[8], (DEPTH, A_WIDTH), 0.02),
        'gn_b': 1.0 + nrm(ks[9], (DEPTH, B_WIDTH), 0.02),
        'w_out': nrm(ks[10], (DEPTH, MIX_WIDTH, D), DN_BETA * MIX_WIDTH ** -0.5),
        'ln1_g': 1.0 + nrm(ks[11], (DEPTH, D), 0.02),
        'ln1_b': nrm(ks[12], (DEPTH, D), 0.02),
        'w_query': nrm(ks[13], (DEPTH, D, PEER_HEADS * QUERY_DIM), D ** -0.5),
        'sub_keys': nrm(ks[14], (DEPTH, PEER_HEADS, 2, N_KEYS, KEY_HALF), KEY_HALF ** -0.5),
        'expert_down': nrm(ks[15], (DEPTH, N_EXPERTS, D), D ** -0.5),
        'expert_up': nrm(ks[16], (DEPTH, N_EXPERTS, D), DN_BETA),
        'ln2_g': 1.0 + nrm(ks[17], (DEPTH, D), 0.02),
        'ln2_b': nrm(ks[18], (DEPTH, D), 0.02),
    }


def reference(x_prompt, x_sample, c_prompt, c_sample, w_mod, b_mod, w_in, attn_sink, gn_a, gn_b, w_out,
              ln1_g, ln1_b, w_query, sub_keys, expert_down, expert_up, ln2_g, ln2_b):
    def trunk(x, c):
        for l in range(DEPTH):
            x = _encoder_layer(x, c, w_mod[l], b_mod[l], w_in[l], attn_sink[l], gn_a[l], gn_b[l], w_out[l],
                               ln1_g[l], ln1_b[l], w_query[l], sub_keys[l], expert_down[l], expert_up[l],
                               ln2_g[l], ln2_b[l])
        return x

    y_prompt = trunk(x_prompt, c_prompt)
    y_sample = trunk(x_sample, c_sample)
    return (y_prompt, y_sample)
```

```python
import functools

import jax
import jax.numpy as jnp
import numpy as np
from jax import lax
from jax.experimental import pallas as pl
from jax.experimental.pallas import tpu as pltpu

F32 = jnp.float32
BF16 = jnp.bfloat16

LANES = 128
SUBLANES = 8
VMEM_LIMIT = 56 * 1024 * 1024

HEAD_DIM = 64
A_HEADS = 8
B_Q_HEADS = 8
B_KV_HEADS = 2
A_WIDTH = A_HEADS * HEAD_DIM
B_WIDTH = B_Q_HEADS * HEAD_DIM
KV_B_WIDTH = B_KV_HEADS * HEAD_DIM
IN_WIDTH = 3 * A_WIDTH + B_WIDTH + 2 * KV_B_WIDTH
DILATION_PAIRS = ((128, 1), (512, 4), (2048, 16))
B_RADIUS = 128
ROT_DIM = HEAD_DIM // 4
ROPE_THETA = 500000.0
N_KEYS = 128
PEER_HEADS = 8
PEER_TOPK = 16
LN_EPS = 1e-5
RMS_EPS = 1e-6
DEPTH = 1
DN_ALPHA = (2 * DEPTH) ** 0.25

NT_DIMS = (((1,), (1,)), ((), ()))


def _cparams(*sem):
    return pltpu.CompilerParams(dimension_semantics=sem, vmem_limit_bytes=VMEM_LIMIT)


def _mod_kernel(c_ref, w_ref, b_ref, o_ref):
    c = c_ref[...]
    a = c * jax.nn.sigmoid(c)
    o_ref[...] = jnp.dot(a, w_ref[...], preferred_element_type=F32,
                         precision=lax.Precision.HIGHEST) + b_ref[...]


def _modulation(c, w_mod, b_mod):
    n, d = c.shape
    width = w_mod.shape[1]
    tn = 1536
    return pl.pallas_call(
        _mod_kernel,
        grid=(width // tn,),
        in_specs=[pl.BlockSpec((n, d), lambda j: (0, 0)),
                  pl.BlockSpec((d, tn), lambda j: (0, j)),
                  pl.BlockSpec((1, tn), lambda j: (0, j))],
        out_specs=pl.BlockSpec((n, tn), lambda j: (0, j)),
        out_shape=jax.ShapeDtypeStruct((n, width), F32),
        compiler_params=_cparams("arbitrary"),
        name="mod_kernel",
    )(c, w_mod, b_mod.reshape(1, width))


def _rope_tables(seq):
    half = ROT_DIM // 2
    inv = ROPE_THETA ** (-jnp.arange(half, dtype=F32) * 2.0 / ROT_DIM)
    ang = jnp.arange(seq, dtype=F32)[:, None] * inv[None, :]
    cos, sin = jnp.cos(ang), jnp.sin(ang)
    ones = jnp.ones((seq, HEAD_DIM - ROT_DIM), F32)
    zeros = jnp.zeros((seq, HEAD_DIM - ROT_DIM), F32)
    zh = jnp.zeros((seq, half), F32)
    c_head = jnp.concatenate([cos, cos, ones], axis=-1)
    s1_head = jnp.concatenate([zh, sin, zeros], axis=-1)
    s2_head = jnp.concatenate([-sin, zh, zeros], axis=-1)
    two = lambda t: jnp.concatenate([t, t], axis=-1)
    return two(c_head), two(s1_head), two(s2_head)


def _qkv_kernel(x_ref, mod_ref, w_ref, c_ref, s1_ref, s2_ref,
                qa_ref, ka_ref, va_ref, qb_ref, kb_ref, vb_ref):
    x = x_ref[0]
    sh1 = mod_ref[0, 0:1, :]
    sc1 = mod_ref[0, 1:2, :]
    h = (x * (1.0 + sc1) + sh1).astype(BF16)
    proj = jnp.dot(h, w_ref[...], preferred_element_type=F32)
    cs, s1, s2 = c_ref[...], s1_ref[...], s2_ref[...]
    half = ROT_DIM // 2

    def rope(col):
        t = proj[:, col:col + LANES]
        return t * cs + pltpu.roll(t, half, 1) * s1 + pltpu.roll(t, LANES - half, 1) * s2

    for j in range(A_WIDTH // LANES):
        sl = slice(j * LANES, (j + 1) * LANES)
        qa_ref[0, :, sl] = rope(j * LANES).astype(BF16)
        ka_ref[0, :, sl] = rope(A_WIDTH + j * LANES).astype(BF16)
        va_ref[0, :, sl] = proj[:, 2 * A_WIDTH + j * LANES:2 * A_WIDTH + (j + 1) * LANES].astype(BF16)
        qb_ref[0, :, sl] = rope(3 * A_WIDTH + j * LANES).astype(BF16)
    kb = rope(3 * A_WIDTH + B_WIDTH)
    vb = proj[:, 3 * A_WIDTH + B_WIDTH + KV_B_WIDTH:]
    lane = lax.broadcasted_iota(jnp.int32, (1, LANES), 1)
    low = lane < HEAD_DIM
    for src, dst in ((kb, kb_ref), (vb, vb_ref)):
        sw = pltpu.roll(src, HEAD_DIM, 1)
        dst[0, :, 0:LANES] = jnp.where(low, src, sw).astype(BF16)
        dst[0, :, LANES:2 * LANES] = jnp.where(low, sw, src).astype(BF16)


def _qkv(x, mod, w_in_bf, tables, tm=512):
    b, s, d = x.shape
    full = lambda shape: pl.BlockSpec(shape, lambda i, j: (0,) * len(shape))
    tok = lambda w: pl.BlockSpec((1, tm, w), lambda i, j: (i, j, 0))
    tab = pl.BlockSpec((tm, LANES), lambda i, j: (j, 0))
    outs = [jax.ShapeDtypeStruct((b, s, w), BF16)
            for w in (A_WIDTH, A_WIDTH, A_WIDTH, B_WIDTH, 2 * KV_B_WIDTH, 2 * KV_B_WIDTH)]
    return pl.pallas_call(
        _qkv_kernel,
        grid=(b, s // tm),
        in_specs=[tok(d), pl.BlockSpec((1, 6, d), lambda i, j: (i, 0, 0)),
                  full((d, IN_WIDTH)), tab, tab, tab],
        out_specs=[tok(o.shape[-1]) for o in outs],
        out_shape=outs,
        compiler_params=_cparams("parallel", "arbitrary"),
        name="qkv_kernel",
    )(x, mod, w_in_bf, *tables)


def _band_kernel(*refs, length, radius, tq, tk, has_sink, want_lse):
    if has_sink:
        sink_ref, q_ref, k_ref, v_ref = refs[:4]
        outs = refs[4:]
    else:
        q_ref, k_ref, v_ref = refs[:3]
        outs = refs[3:]
    o_ref = outs[0]
    lse_ref = outs[1] if want_lse else None
    hp = pl.program_id(2)
    lane = lax.broadcasted_iota(jnp.int32, (1, LANES), 1)
    low = lane < HEAD_DIM
    qi = lax.broadcasted_iota(jnp.int32, (tq, 1), 0)
    ki = lax.broadcasted_iota(jnp.int32, (1, tk), 1)

    def body(i, carry):
        q0 = pl.multiple_of(i * tq, tq)
        start = pl.multiple_of(jnp.clip(q0 - radius, 0, length - tk), 64)
        q = q_ref[0, pl.ds(q0, tq), :]
        k = k_ref[0, pl.ds(start, tk), :]
        v = v_ref[0, pl.ds(start, tk), :]
        valid = jnp.abs((start + ki) - (q0 + qi)) <= radius
        o_heads, lse_heads = [], []
        for hh in range(2):
            sel = low if hh == 0 else jnp.logical_not(low)
            qh = jnp.where(sel, q, jnp.zeros_like(q))
            s = lax.dot_general(qh, k, NT_DIMS, preferred_element_type=F32)
            s = jnp.where(valid, s, -jnp.inf)
            m = jnp.max(s, axis=-1, keepdims=True)
            if has_sink:
                sk = sink_ref[hp * 2 + hh]
                m = jnp.maximum(m, sk)
            p = jnp.exp(s - m)
            den = jnp.sum(p, axis=-1, keepdims=True)
            if has_sink:
                den = den + jnp.exp(sk - m)
            o = jnp.dot(p.astype(BF16), v, preferred_element_type=F32)
            o_heads.append(o / den)
            lse_heads.append(m + jnp.log(den))
        o_ref[0, pl.ds(q0, tq), :] = jnp.where(low, o_heads[0], o_heads[1]).astype(o_ref.dtype)
        if want_lse:
            lse_ref[0, pl.ds(q0, tq), :] = jnp.where(low, lse_heads[0], lse_heads[1])
        return carry

    lax.fori_loop(0, length // tq, body, 0)


def _band_attention(q, k, v, *, dilation, radius, kv_shared, sink=None, want_lse=False):
    b, s, width = q.shape
    d = dilation
    length = s // d
    n_hp = width // LANES
    tq = min(128, length)
    tk = min(tq + 2 * radius, length)
    qv = q.reshape(b, length, d * width)
    kw = k.shape[-1]
    kv_blocks = kw // LANES
    kview = k.reshape(b, length, d * kw)
    vview = v.reshape(b, length, d * kw)
    q_spec = pl.BlockSpec((1, length, LANES), lambda i, r, h: (i, 0, r * n_hp + h))
    if kv_shared:
        k_spec = pl.BlockSpec((1, length, LANES), lambda i, r, h: (i, 0, r * kv_blocks + h // 2))
    else:
        k_spec = q_spec
    in_specs = [q_spec, k_spec, k_spec]
    args = [qv, kview, vview]
    if sink is not None:
        in_specs = [pl.BlockSpec(memory_space=pltpu.SMEM)] + in_specs
        args = [sink] + args
    out_shape = [jax.ShapeDtypeStruct(qv.shape, BF16)]
    out_specs = [q_spec]
    if want_lse:
        out_shape.append(jax.ShapeDtypeStruct(qv.shape, F32))
        out_specs.append(q_spec)
    kern = functools.partial(_band_kernel, length=length, radius=radius, tq=tq, tk=tk,
                             has_sink=sink is not None, want_lse=want_lse)
    res = pl.pallas_call(
        kern,
        grid=(b, d, n_hp),
        in_specs=in_specs,
        out_specs=out_specs,
        out_shape=out_shape,
        compiler_params=_cparams("parallel", "arbitrary", "arbitrary"),
        name=f"band_attn_d{d}_r{radius}",
    )(*args)
    return [t.reshape(b, s, width) for t in res]


def _layer_norm_rows(y, g, bias):
    mu = jnp.mean(y, axis=-1, keepdims=True)
    yc = y - mu
    var = jnp.mean(yc * yc, axis=-1, keepdims=True)
    return yc * lax.rsqrt(var + LN_EPS) * g + bias


def _post_kernel(o1_ref, o4_ref, o16_ref, l1_ref, l4_ref, l16_ref, ob_ref, x_ref, mod_ref,
                 gna_ref, gnb_ref, w_ref, g_ref, b_ref, x1_ref, h2_ref):
    l1, l4, l16 = l1_ref[0], l4_ref[0], l16_ref[0]
    mx = jnp.maximum(jnp.maximum(l1, l4), l16)
    e1, e4, e16 = jnp.exp(l1 - mx), jnp.exp(l4 - mx), jnp.exp(l16 - mx)
    num = (e1 * o1_ref[0].astype(F32) + e4 * o4_ref[0].astype(F32) + e16 * o16_ref[0].astype(F32))
    oa = num / (e1 + e4 + e16)
    ob = ob_ref[0].astype(F32)

    def rms(t, g):
        return t * lax.rsqrt(jnp.mean(t * t, axis=-1, keepdims=True) + RMS_EPS) * g

    merged = jnp.concatenate([rms(oa, gna_ref[...]), rms(ob, gnb_ref[...])], axis=-1).astype(BF16)
    mix = jnp.dot(merged, w_ref[...], preferred_element_type=F32)
    gt1 = mod_ref[0, 2:3, :]
    sh2 = mod_ref[0, 3:4, :]
    sc2 = mod_ref[0, 4:5, :]
    x1 = _layer_norm_rows(DN_ALPHA * x_ref[0] + (1.0 + gt1) * mix, g_ref[...], b_ref[...])
    x1_ref[0] = x1
    h2_ref[0] = (x1 * (1.0 + sc2) + sh2).astype(BF16)


def _post(o1, o4, o16, l1, l4, l16, ob, x, mod, gn_a, gn_b, w_out_bf, ln_g, ln_b, tm=256):
    b, s, d = x.shape
    tok = lambda w: pl.BlockSpec((1, tm, w), lambda i, j: (i, j, 0))
    full = lambda shape: pl.BlockSpec(shape, lambda i, j: (0,) * len(shape))
    return pl.pallas_call(
        _post_kernel,
        grid=(b, s // tm),
        in_specs=[tok(A_WIDTH)] * 3 + [tok(A_WIDTH)] * 3 + [tok(B_WIDTH), tok(d),
                  pl.BlockSpec((1, 6, d), lambda i, j: (i, 0, 0)),
                  full((1, A_WIDTH)), full((1, B_WIDTH)), full((d, d)), full((1, d)), full((1, d))],
        out_specs=[tok(d), tok(d)],
        out_shape=[jax.ShapeDtypeStruct((b, s, d), F32), jax.ShapeDtypeStruct((b, s, d), BF16)],
        compiler_params=_cparams("parallel", "arbitrary"),
        name="post_kernel",
    )(o1, o4, o16, l1, l4, l16, ob, x, mod, gn_a.reshape(1, -1), gn_b.reshape(1, -1),
      w_out_bf, ln_g.reshape(1, -1), ln_b.reshape(1, -1))


NEG = -jnp.inf


def _lex_allmax(val, idx):
    for shift in (4, 2, 1):
        v2 = pltpu.roll(val, shift, 0)
        i2 = pltpu.roll(idx, shift, 0)
        better = (v2 > val) | ((v2 == val) & (i2 < idx))
        val = jnp.where(better, v2, val)
        idx = jnp.where(better, i2, idx)
    return val, idx


def _extract_top(vals, ids, n_top):
    ranks = [jnp.full(v.shape, float(n_top), F32) for v in vals]
    tops = []
    for kk in range(n_top):
        bv, bi = vals[0], ids[0]
        for v, i in zip(vals[1:], ids[1:]):
            gt = v > bv
            bv = jnp.where(gt, v, bv)
            bi = jnp.where(gt, i, bi)
        bv, bi = _lex_allmax(bv, bi)
        tops.append(bv)
        new_vals, new_ranks = [], []
        for v, i, r in zip(vals, ids, ranks):
            hit = i == bi
            new_vals.append(jnp.where(hit, NEG, v))
            new_ranks.append(jnp.where(hit, float(kk), r))
        vals, ranks = new_vals, new_ranks
    return ranks, tops


def _sublane_sum(x):
    for shift in (4, 2, 1):
        x = x + pltpu.roll(x, shift, 0)
    return x


def _pack_pair(x):
    bits = pltpu.bitcast(x.astype(BF16).astype(F32), jnp.uint32)
    return bits | (bits >> 16)


def _select_kernel(h2_ref, wq_ref, keys_ref, a1_ref, c1_ref, e2_ref, r2_ref, s_scr, *, n_lb):
    h2 = h2_ref[...]
    for hp in range(2 * PEER_HEADS):
        wq = wq_ref[hp * LANES:(hp + 1) * LANES, :]
        q_t = lax.dot_general(wq, h2, NT_DIMS, preferred_element_type=F32)
        s_t = jnp.dot(keys_ref[hp], q_t.astype(BF16), preferred_element_type=F32)
        for lb in range(n_lb):
            s_scr[hp, lb] = s_t[:, lb * LANES:(lb + 1) * LANES]

    n_vr = N_KEYS // SUBLANES
    sub = lax.broadcasted_iota(jnp.int32, (SUBLANES, LANES), 0)
    key_ids = [sub + v * SUBLANES for v in range(n_vr)]
    subf = sub.astype(F32)
    stair_limit = [None, None] + [PEER_TOPK // (a + 1) for a in range(1, 8)] + [None]
    cand_ids = ([sub, sub + SUBLANES] + [sub + a * PEER_TOPK for a in range(1, 8)]
                + [(sub + SUBLANES) * PEER_TOPK])

    def body(t, carry):
        h = t // n_lb
        lb = t % n_lb
        halves = []
        for p in range(2):
            s_full = s_scr[2 * h + p, lb]
            vals = [s_full[v * SUBLANES:(v + 1) * SUBLANES, :] for v in range(n_vr)]
            ranks, tops = _extract_top(vals, key_ids, PEER_TOPK)
            halves.append((vals, ranks, tops))
        (v1, r1, t1), (v2, r2, t2) = halves
        def by_sublane(tops, base):
            out = jnp.zeros((SUBLANES, LANES), F32)
            for j in range(SUBLANES):
                out = jnp.where(sub == j, tops[base + j], out)
            return out
        t2_lo, t2_hi = by_sublane(t2, 0), by_sublane(t2, SUBLANES)
        t1_hi = by_sublane(t1, SUBLANES)
        cands = [t1[0] + t2_lo, t1[0] + t2_hi]
        for a in range(1, 8):
            cands.append(jnp.where(sub < stair_limit[a + 1], t1[a] + t2_lo, NEG))
        cands.append(t1_hi + t2[0])
        cranks, _ = _extract_top(cands, cand_ids, PEER_TOPK)
        picked = [(r < float(PEER_TOPK)).astype(F32) for r in cranks]
        counts = [_sublane_sum(picked[0] + picked[1])]
        for a in range(1, 8):
            counts.append(_sublane_sum(picked[a + 1]))
        for a in range(8, PEER_TOPK):
            counts.append(_sublane_sum(jnp.where(sub == a - 8, picked[9], 0.0)))
        vmax = t1[0] + t2[0]
        z = jnp.zeros((SUBLANES, LANES), F32)
        for cv, pk in zip(cands, picked):
            z = z + pk * jnp.exp(jnp.where(pk > 0, cv, vmax) - vmax)
        z = _sublane_sum(z)
        inv_z = 1.0 / z
        for v in range(n_vr):
            rows = pl.ds(v * SUBLANES, SUBLANES)
            c1 = jnp.zeros((SUBLANES, LANES), F32)
            for a in range(PEER_TOPK):
                c1 = jnp.where(r1[v] == float(a), counts[a], c1)
            a1 = jnp.exp(v1[v] - t1[0]) * inv_z
            a1_ref[h, lb, rows, :] = _pack_pair(a1)
            c1_ref[h, lb, rows, :] = _pack_pair(c1)
        for vv in range(n_vr // 2):
            rows = pl.ds(vv * 2 * SUBLANES, 2 * SUBLANES)
            e2 = jnp.concatenate([jnp.exp(v2[2 * vv] - t2[0]), jnp.exp(v2[2 * vv + 1] - t2[0])], axis=0)
            rr = jnp.concatenate([r2[2 * vv], r2[2 * vv + 1]], axis=0)
            e2_ref[h, lb, rows, :] = e2.astype(BF16)
            r2_ref[h, lb, rows, :] = rr.astype(BF16)
        return carry

    lax.fori_loop(0, PEER_HEADS * n_lb, body, 0)


def _peer_select(h2, wq_t_bf, keys_bf, tt=256):
    n, d = h2.shape
    n_lb = tt // LANES
    blk = pl.BlockSpec((PEER_HEADS, n_lb, N_KEYS, LANES), lambda i: (0, i, 0, 0))
    shape = (PEER_HEADS, n // LANES, N_KEYS, LANES)
    return pl.pallas_call(
        functools.partial(_select_kernel, n_lb=n_lb),
        grid=(n // tt,),
        in_specs=[pl.BlockSpec((tt, d), lambda i: (i, 0)),
                  pl.BlockSpec(wq_t_bf.shape, lambda i: (0, 0)),
                  pl.BlockSpec(keys_bf.shape, lambda i: (0, 0, 0))],
        out_specs=[blk, blk, blk, blk],
        out_shape=[jax.ShapeDtypeStruct(shape, jnp.uint32), jax.ShapeDtypeStruct(shape, jnp.uint32),
                   jax.ShapeDtypeStruct(shape, BF16), jax.ShapeDtypeStruct(shape, BF16)],
        scratch_shapes=[pltpu.VMEM((2 * PEER_HEADS, n_lb, N_KEYS, LANES), F32)],
        compiler_params=_cparams("parallel"),
        name="peer_select",
    )(h2, wq_t_bf, keys_bf)


def _gelu_exact(x):
    return 0.5 * x * (1.0 + lax.erf(x * (2.0 ** -0.5)))


def _dense_kernel(h2_ref, x1_ref, mod_ref, a1_ref, c1_ref, e2_ref, r2_ref, down_ref, up_ref,
                  g_ref, b_ref, y_ref, acc_ref, act_ref, s_ref, *, n_lb, eb):
    j = pl.program_id(1)

    @pl.when(j == 0)
    def _():
        acc_ref[...] = jnp.zeros_like(acc_ref)

    d_t = lax.dot_general(down_ref[...], h2_ref[...], NT_DIMS, preferred_element_type=F32)
    act_ref[...] = _gelu_exact(d_t).astype(BF16)
    rows_per_key = N_KEYS
    n_i1 = eb // rows_per_key
    chunk = 2 * SUBLANES
    n_chunks = rows_per_key // chunk
    group = 4

    def body(il, carry):
        i1 = j * n_i1 + il
        for lb in range(n_lb):
            lanes = slice(lb * LANES, (lb + 1) * LANES)
            for g0 in range(0, n_chunks, group):
                accs = [jnp.zeros((chunk, LANES), BF16) for _ in range(group)]
                for h in range(PEER_HEADS):
                    a1 = pltpu.bitcast(jnp.broadcast_to(a1_ref[h, lb, pl.ds(i1, 1), :], (SUBLANES, LANES)), BF16)
                    c1 = pltpu.bitcast(jnp.broadcast_to(c1_ref[h, lb, pl.ds(i1, 1), :], (SUBLANES, LANES)), BF16)
                    for g in range(group):
                        rows = pl.ds((g0 + g) * chunk, chunk)
                        e2 = e2_ref[h, lb, rows, :]
                        r2 = r2_ref[h, lb, rows, :]
                        accs[g] = accs[g] + jnp.where(r2 < c1, e2 * a1, jnp.zeros_like(e2))
                for g in range(group):
                    rows = pl.ds(pl.multiple_of(il * rows_per_key, rows_per_key) + (g0 + g) * chunk, chunk)
                    s_ref[rows, lanes] = accs[g] * act_ref[rows, lanes]
        return carry

    lax.fori_loop(0, n_i1, body, 0)
    acc_ref[...] += jnp.dot(up_ref[...], s_ref[...], preferred_element_type=F32)

    @pl.when(j == pl.num_programs(1) - 1)
    def _():
        peer = acc_ref[...].T
        gt2 = mod_ref[0, 5:6, :]
        y_ref[...] = _layer_norm_rows(DN_ALPHA * x1_ref[...] + (1.0 + gt2) * peer, g_ref[...], b_ref[...])


def _peer_dense(h2, x1, mod, sel, down_bf, up_t_bf, ln_g, ln_b, seq, tt=512, eb=1024):
    n, d = h2.shape
    n_exp = down_bf.shape[0]
    n_lb = tt // LANES
    tok = pl.BlockSpec((tt, d), lambda i, j: (i, 0))
    sblk = pl.BlockSpec((PEER_HEADS, n_lb, N_KEYS, LANES), lambda i, j: (0, i, 0, 0))
    vec = pl.BlockSpec((1, d), lambda i, j: (0, 0))
    return pl.pallas_call(
        functools.partial(_dense_kernel, n_lb=n_lb, eb=eb),
        grid=(n // tt, n_exp // eb),
        in_specs=[tok, tok, pl.BlockSpec((1, 6, d), lambda i, j: (i * tt // seq, 0, 0)),
                  sblk, sblk, sblk, sblk,
                  pl.BlockSpec((eb, d), lambda i, j: (j, 0)),
                  pl.BlockSpec((d, eb), lambda i, j: (0, j)),
                  vec, vec],
        out_specs=tok,
        out_shape=jax.ShapeDtypeStruct((n, d), F32),
        scratch_shapes=[pltpu.VMEM((d, tt), F32), pltpu.VMEM((eb, tt), BF16), pltpu.VMEM((eb, tt), BF16)],
        compiler_params=_cparams("parallel", "arbitrary"),
        name="peer_dense",
    )(h2, x1, mod, *sel, down_bf, up_t_bf, ln_g.reshape(1, -1), ln_b.reshape(1, -1))


def _encoder_layer(x, mod, weights):
    (w_in_bf, sink, gn_a, gn_b, w_out_bf, ln1_g, ln1_b, wq_t_bf, keys_bf, down_bf, up_t_bf,
     ln2_g, ln2_b) = weights
    b, s, d = x.shape
    tables = _rope_tables(s)
    qa, ka, va, qb, kb2, vb2 = _qkv(x, mod, w_in_bf, tables)
    branch = []
    for window, dil in DILATION_PAIRS:
        branch.append(_band_attention(qa, ka, va, dilation=dil, radius=window // (2 * dil),
                                      kv_shared=False, want_lse=True))
    (ob,) = _band_attention(qb, kb2, vb2, dilation=1, radius=B_RADIUS, kv_shared=True, sink=sink)
    (o1, l1), (o4, l4), (o16, l16) = branch
    x1, h2 = _post(o1, o4, o16, l1, l4, l16, ob, x, mod, gn_a, gn_b, w_out_bf, ln1_g, ln1_b)
    h2f = h2.reshape(b * s, d)
    sel = _peer_select(h2f, wq_t_bf, keys_bf)
    y = _peer_dense(h2f, x1.reshape(b * s, d), mod, sel, down_bf, up_t_bf, ln2_g, ln2_b, s)
    return y.reshape(b, s, d)


def kernel(x_prompt, x_sample, c_prompt, c_sample, w_mod, b_mod, w_in, attn_sink, gn_a, gn_b, w_out,
           ln1_g, ln1_b, w_query, sub_keys, expert_down, expert_up, ln2_g, ln2_b):
    l = 0
    d = x_prompt.shape[-1]
    nb_p = c_prompt.shape[0]
    c_all = jnp.concatenate([c_prompt, c_sample], axis=0)
    mod = _modulation(c_all, w_mod[l], b_mod[l]).reshape(c_all.shape[0], 6, d)
    col = np.ones((IN_WIDTH,), np.float32)
    col[:A_WIDTH] = HEAD_DIM ** -0.5
    col[3 * A_WIDTH:3 * A_WIDTH + B_WIDTH] = HEAD_DIM ** -0.5
    weights = (
        (w_in[l] * col[None, :]).astype(BF16), attn_sink[l], gn_a[l], gn_b[l], w_out[l].astype(BF16),
        ln1_g[l], ln1_b[l], w_query[l].T.astype(BF16),
        sub_keys[l].reshape(2 * PEER_HEADS, N_KEYS, -1).astype(BF16),
        expert_down[l].astype(BF16), expert_up[l].T.astype(BF16), ln2_g[l], ln2_b[l])
    y_prompt = _encoder_layer(x_prompt, mod[:nb_p], weights)
    y_sample = _encoder_layer(x_sample, mod[nb_p:], weights)
    return (y_prompt, y_sample)
```

```python
import functools

import jax
import jax.numpy as jnp
import numpy as np
from jax import lax
from jax.experimental import pallas as pl
from jax.experimental.pallas import tpu as pltpu

F32 = jnp.float32
BF16 = jnp.bfloat16

LANES = 128
SUBLANES = 8
VMEM_LIMIT = 56 * 1024 * 1024

HEAD_DIM = 64
A_HEADS = 8
B_Q_HEADS = 8
B_KV_HEADS = 2
A_WIDTH = A_HEADS * HEAD_DIM
B_WIDTH = B_Q_HEADS * HEAD_DIM
KV_B_WIDTH = B_KV_HEADS * HEAD_DIM
IN_WIDTH = 3 * A_WIDTH + B_WIDTH + 2 * KV_B_WIDTH
DILATION_PAIRS = ((128, 1), (512, 4), (2048, 16))
B_RADIUS = 128
ROT_DIM = HEAD_DIM // 4
ROPE_THETA = 500000.0
N_KEYS = 128
PEER_HEADS = 8
PEER_TOPK = 16
LN_EPS = 1e-5
RMS_EPS = 1e-6
DEPTH = 1
DN_ALPHA = (2 * DEPTH) ** 0.25

NT_DIMS = (((1,), (1,)), ((), ()))


def _cparams(*sem):
    return pltpu.CompilerParams(dimension_semantics=sem, vmem_limit_bytes=VMEM_LIMIT)


def _mod_kernel(c_ref, w_ref, b_ref, o_ref):
    c = c_ref[...]
    a = c * jax.nn.sigmoid(c)
    o_ref[...] = jnp.dot(a, w_ref[...], preferred_element_type=F32,
                         precision=lax.Precision.HIGHEST) + b_ref[...]


def _modulation(c, w_mod, b_mod):
    n, d = c.shape
    width = w_mod.shape[1]
    tn = 1536
    return pl.pallas_call(
        _mod_kernel,
        grid=(width // tn,),
        in_specs=[pl.BlockSpec((n, d), lambda j: (0, 0)),
                  pl.BlockSpec((d, tn), lambda j: (0, j)),
                  pl.BlockSpec((1, tn), lambda j: (0, j))],
        out_specs=pl.BlockSpec((n, tn), lambda j: (0, j)),
        out_shape=jax.ShapeDtypeStruct((n, width), F32),
        compiler_params=_cparams("arbitrary"),
        name="mod_kernel",
    )(c, w_mod, b_mod.reshape(1, width))


def _rope_tables(seq):
    half = ROT_DIM // 2
    inv = ROPE_THETA ** (-jnp.arange(half, dtype=F32) * 2.0 / ROT_DIM)
    ang = jnp.arange(seq, dtype=F32)[:, None] * inv[None, :]
    cos, sin = jnp.cos(ang), jnp.sin(ang)
    ones = jnp.ones((seq, HEAD_DIM - ROT_DIM), F32)
    zeros = jnp.zeros((seq, HEAD_DIM - ROT_DIM), F32)
    zh = jnp.zeros((seq, half), F32)
    c_head = jnp.concatenate([cos, cos, ones], axis=-1)
    s1_head = jnp.concatenate([zh, sin, zeros], axis=-1)
    s2_head = jnp.concatenate([-sin, zh, zeros], axis=-1)
    two = lambda t: jnp.concatenate([t, t], axis=-1)
    return two(c_head), two(s1_head), two(s2_head)


def _qkv_kernel(x_ref, mod_ref, w_ref, c_ref, s1_ref, s2_ref,
                qa_ref, ka_ref, va_ref, qb_ref, kb_ref, vb_ref):
    x = x_ref[0]
    sh1 = mod_ref[0, 0:1, :]
    sc1 = mod_ref[0, 1:2, :]
    h = (x * (1.0 + sc1) + sh1).astype(BF16)
    proj = jnp.dot(h, w_ref[...], preferred_element_type=F32)
    cs, s1, s2 = c_ref[...], s1_ref[...], s2_ref[...]
    half = ROT_DIM // 2

    def rope(col):
        t = proj[:, col:col + LANES]
        return t * cs + pltpu.roll(t, half, 1) * s1 + pltpu.roll(t, LANES - half, 1) * s2

    for j in range(A_WIDTH // LANES):
        sl = slice(j * LANES, (j + 1) * LANES)
        qa_ref[0, :, sl] = rope(j * LANES).astype(BF16)
        ka_ref[0, :, sl] = rope(A_WIDTH + j * LANES).astype(BF16)
        va_ref[0, :, sl] = proj[:, 2 * A_WIDTH + j * LANES:2 * A_WIDTH + (j + 1) * LANES].astype(BF16)
        qb_ref[0, :, sl] = rope(3 * A_WIDTH + j * LANES).astype(BF16)
    kb = rope(3 * A_WIDTH + B_WIDTH)
    vb = proj[:, 3 * A_WIDTH + B_WIDTH + KV_B_WIDTH:]
    lane = lax.broadcasted_iota(jnp.int32, (1, LANES), 1)
    low = lane < HEAD_DIM
    for src, dst in ((kb, kb_ref), (vb, vb_ref)):
        sw = pltpu.roll(src, HEAD_DIM, 1)
        dst[0, :, 0:LANES] = jnp.where(low, src, sw).astype(BF16)
        dst[0, :, LANES:2 * LANES] = jnp.where(low, sw, src).astype(BF16)


def _qkv(x, mod, w_in_bf, tables, tm=512):
    b, s, d = x.shape
    full = lambda shape: pl.BlockSpec(shape, lambda i, j: (0,) * len(shape))
    tok = lambda w: pl.BlockSpec((1, tm, w), lambda i, j: (i, j, 0))
    tab = pl.BlockSpec((tm, LANES), lambda i, j: (j, 0))
    outs = [jax.ShapeDtypeStruct((b, s, w), BF16)
            for w in (A_WIDTH, A_WIDTH, A_WIDTH, B_WIDTH, 2 * KV_B_WIDTH, 2 * KV_B_WIDTH)]
    return pl.pallas_call(
        _qkv_kernel,
        grid=(b, s // tm),
        in_specs=[tok(d), pl.BlockSpec((1, 6, d), lambda i, j: (i, 0, 0)),
                  full((d, IN_WIDTH)), tab, tab, tab],
        out_specs=[tok(o.shape[-1]) for o in outs],
        out_shape=outs,
        compiler_params=_cparams("parallel", "arbitrary"),
        name="qkv_kernel",
    )(x, mod, w_in_bf, *tables)


def _band_kernel(*refs, length, radius, tq, tk, has_sink, want_lse):
    if has_sink:
        sink_ref, q_ref, k_ref, v_ref = refs[:4]
        outs = refs[4:]
    else:
        q_ref, k_ref, v_ref = refs[:3]
        outs = refs[3:]
    o_ref = outs[0]
    lse_ref = outs[1] if want_lse else None
    hp = pl.program_id(2)
    lane = lax.broadcasted_iota(jnp.int32, (1, LANES), 1)
    low = lane < HEAD_DIM
    qi = lax.broadcasted_iota(jnp.int32, (tq, 1), 0)
    ki = lax.broadcasted_iota(jnp.int32, (1, tk), 1)

    def body(i, carry):
        q0 = pl.multiple_of(i * tq, tq)
        start = pl.multiple_of(jnp.clip(q0 - radius, 0, length - tk), 64)
        q = q_ref[0, pl.ds(q0, tq), :]
        k = k_ref[0, pl.ds(start, tk), :]
        v = v_ref[0, pl.ds(start, tk), :]
        valid = jnp.abs((start + ki) - (q0 + qi)) <= radius
        o_heads, lse_heads = [], []
        for hh in range(2):
            sel = low if hh == 0 else jnp.logical_not(low)
            qh = jnp.where(sel, q, jnp.zeros_like(q))
            s = lax.dot_general(qh, k, NT_DIMS, preferred_element_type=F32)
            s = jnp.where(valid, s, -jnp.inf)
            m = jnp.max(s, axis=-1, keepdims=True)
            if has_sink:
                sk = sink_ref[hp * 2 + hh]
                m = jnp.maximum(m, sk)
            p = jnp.exp(s - m)
            den = jnp.sum(p, axis=-1, keepdims=True)
            if has_sink:
                den = den + jnp.exp(sk - m)
            o = jnp.dot(p.astype(BF16), v, preferred_element_type=F32)
            o_heads.append(o / den)
            lse_heads.append(m + jnp.log(den))
        o_ref[0, pl.ds(q0, tq), :] = jnp.where(low, o_heads[0], o_heads[1]).astype(o_ref.dtype)
        if want_lse:
            lse_ref[0, pl.ds(q0, tq), :] = jnp.where(low, lse_heads[0], lse_heads[1])
        return carry

    lax.fori_loop(0, length // tq, body, 0)


def _band_attention(q, k, v, *, dilation, radius, kv_shared, sink=None, want_lse=False):
    b, s, width = q.shape
    d = dilation
    length = s // d
    n_hp = width // LANES
    tq = min(128, length)
    tk = min(tq + 2 * radius, length)
    qv = q.reshape(b, length, d * width)
    kw = k.shape[-1]
    kv_blocks = kw // LANES
    kview = k.reshape(b, length, d * kw)
    vview = v.reshape(b, length, d * kw)
    q_spec = pl.BlockSpec((1, length, LANES), lambda i, r, h: (i, 0, r * n_hp + h))
    if kv_shared:
        k_spec = pl.BlockSpec((1, length, LANES), lambda i, r, h: (i, 0, r * kv_blocks + h // 2))
    else:
        k_spec = q_spec
    in_specs = [q_spec, k_spec, k_spec]
    args = [qv, kview, vview]
    if sink is not None:
        in_specs = [pl.BlockSpec(memory_space=pltpu.SMEM)] + in_specs
        args = [sink] + args
    out_shape = [jax.ShapeDtypeStruct(qv.shape, BF16)]
    out_specs = [q_spec]
    if want_lse:
        out_shape.append(jax.ShapeDtypeStruct(qv.shape, F32))
        out_specs.append(q_spec)
    kern = functools.partial(_band_kernel, length=length, radius=radius, tq=tq, tk=tk,
                             has_sink=sink is not None, want_lse=want_lse)
    res = pl.pallas_call(
        kern,
        grid=(b, d, n_hp),
        in_specs=in_specs,
        out_specs=out_specs,
        out_shape=out_shape,
        compiler_params=_cparams("parallel", "arbitrary", "arbitrary"),
        name=f"band_attn_d{d}_r{radius}",
    )(*args)
    return [t.reshape(b, s, width) for t in res]


def _layer_norm_rows(y, g, bias):
    mu = jnp.mean(y, axis=-1, keepdims=True)
    yc = y - mu
    var = jnp.mean(yc * yc, axis=-1, keepdims=True)
    return yc * lax.rsqrt(var + LN_EPS) * g + bias


def _post_kernel(o1_ref, o4_ref, o16_ref, l1_ref, l4_ref, l16_ref, ob_ref, x_ref, mod_ref,
                 gna_ref, gnb_ref, w_ref, g_ref, b_ref, x1_ref, h2_ref):
    l1, l4, l16 = l1_ref[0], l4_ref[0], l16_ref[0]
    mx = jnp.maximum(jnp.maximum(l1, l4), l16)
    e1, e4, e16 = jnp.exp(l1 - mx), jnp.exp(l4 - mx), jnp.exp(l16 - mx)
    num = (e1 * o1_ref[0].astype(F32) + e4 * o4_ref[0].astype(F32) + e16 * o16_ref[0].astype(F32))
    oa = num / (e1 + e4 + e16)
    ob = ob_ref[0].astype(F32)

    def rms(t, g):
        return t * lax.rsqrt(jnp.mean(t * t, axis=-1, keepdims=True) + RMS_EPS) * g

    merged = jnp.concatenate([rms(oa, gna_ref[...]), rms(ob, gnb_ref[...])], axis=-1).astype(BF16)
    mix = jnp.dot(merged, w_ref[...], preferred_element_type=F32)
    gt1 = mod_ref[0, 2:3, :]
    sh2 = mod_ref[0, 3:4, :]
    sc2 = mod_ref[0, 4:5, :]
    x1 = _layer_norm_rows(DN_ALPHA * x_ref[0] + (1.0 + gt1) * mix, g_ref[...], b_ref[...])
    x1_ref[0] = x1
    h2_ref[0] = (x1 * (1.0 + sc2) + sh2).astype(BF16)


def _post(o1, o4, o16, l1, l4, l16, ob, x, mod, gn_a, gn_b, w_out_bf, ln_g, ln_b, tm=256):
    b, s, d = x.shape
    tok = lambda w: pl.BlockSpec((1, tm, w), lambda i, j: (i, j, 0))
    full = lambda shape: pl.BlockSpec(shape, lambda i, j: (0,) * len(shape))
    return pl.pallas_call(
        _post_kernel,
        grid=(b, s // tm),
        in_specs=[tok(A_WIDTH)] * 3 + [tok(A_WIDTH)] * 3 + [tok(B_WIDTH), tok(d),
                  pl.BlockSpec((1, 6, d), lambda i, j: (i, 0, 0)),
                  full((1, A_WIDTH)), full((1, B_WIDTH)), full((d, d)), full((1, d)), full((1, d))],
        out_specs=[tok(d), tok(d)],
        out_shape=[jax.ShapeDtypeStruct((b, s, d), F32), jax.ShapeDtypeStruct((b, s, d), BF16)],
        compiler_params=_cparams("parallel", "arbitrary"),
        name="post_kernel",
    )(o1, o4, o16, l1, l4, l16, ob, x, mod, gn_a.reshape(1, -1), gn_b.reshape(1, -1),
      w_out_bf, ln_g.reshape(1, -1), ln_b.reshape(1, -1))


NEG = -jnp.inf


def _lex_allmax(val, idx):
    for shift in (4, 2, 1):
        v2 = pltpu.roll(val, shift, 0)
        i2 = pltpu.roll(idx, shift, 0)
        better = (v2 > val) | ((v2 == val) & (i2 < idx))
        val = jnp.where(better, v2, val)
        idx = jnp.where(better, i2, idx)
    return val, idx


def _extract_top(vals, ids, n_top):
    ranks = [jnp.full(v.shape, float(n_top), F32) for v in vals]
    tops = []
    for kk in range(n_top):
        bv, bi = vals[0], ids[0]
        for v, i in zip(vals[1:], ids[1:]):
            gt = v > bv
            bv = jnp.where(gt, v, bv)
            bi = jnp.where(gt, i, bi)
        bv, bi = _lex_allmax(bv, bi)
        tops.append(bv)
        new_vals, new_ranks = [], []
        for v, i, r in zip(vals, ids, ranks):
            hit = i == bi
            new_vals.append(jnp.where(hit, NEG, v))
            new_ranks.append(jnp.where(hit, float(kk), r))
        vals, ranks = new_vals, new_ranks
    return ranks, tops


def _sublane_sum(x):
    for shift in (4, 2, 1):
        x = x + pltpu.roll(x, shift, 0)
    return x


def _bcast_rows_bf16(x):
    return jnp.concatenate([x, x], axis=0).astype(BF16)


def _select_kernel(h2_ref, wq_ref, keys_ref, a1_ref, c1_ref, e2_ref, r2_ref, s_scr, *, n_lb):
    h2 = h2_ref[...]
    for hp in range(2 * PEER_HEADS):
        wq = wq_ref[hp * LANES:(hp + 1) * LANES, :]
        q_t = lax.dot_general(wq, h2, NT_DIMS, preferred_element_type=F32)
        s_t = jnp.dot(keys_ref[hp], q_t.astype(BF16), preferred_element_type=F32)
        for lb in range(n_lb):
            s_scr[hp, lb] = s_t[:, lb * LANES:(lb + 1) * LANES]

    n_vr = N_KEYS // SUBLANES
    sub = lax.broadcasted_iota(jnp.int32, (SUBLANES, LANES), 0)
    key_ids = [sub + v * SUBLANES for v in range(n_vr)]
    subf = sub.astype(F32)
    stair_limit = [None, None] + [PEER_TOPK // (a + 1) for a in range(1, 8)] + [None]
    cand_ids = ([sub, sub + SUBLANES] + [sub + a * PEER_TOPK for a in range(1, 8)]
                + [(sub + SUBLANES) * PEER_TOPK])

    def body(t, carry):
        h = t // n_lb
        lb = t % n_lb
        halves = []
        for p in range(2):
            s_full = s_scr[2 * h + p, lb]
            vals = [s_full[v * SUBLANES:(v + 1) * SUBLANES, :] for v in range(n_vr)]
            ranks, tops = _extract_top(vals, key_ids, PEER_TOPK)
            halves.append((vals, ranks, tops))
        (v1, r1, t1), (v2, r2, t2) = halves
        def by_sublane(tops, base):
            out = jnp.zeros((SUBLANES, LANES), F32)
            for j in range(SUBLANES):
                out = jnp.where(sub == j, tops[base + j], out)
            return out
        t2_lo, t2_hi = by_sublane(t2, 0), by_sublane(t2, SUBLANES)
        t1_hi = by_sublane(t1, SUBLANES)
        cands = [t1[0] + t2_lo, t1[0] + t2_hi]
        for a in range(1, 8):
            cands.append(jnp.where(sub < stair_limit[a + 1], t1[a] + t2_lo, NEG))
        cands.append(t1_hi + t2[0])
        cranks, _ = _extract_top(cands, cand_ids, PEER_TOPK)
        picked = [(r < float(PEER_TOPK)).astype(F32) for r in cranks]
        counts = [_sublane_sum(picked[0] + picked[1])]
        for a in range(1, 8):
            counts.append(_sublane_sum(picked[a + 1]))
        for a in range(8, PEER_TOPK):
            counts.append(_sublane_sum(jnp.where(sub == a - 8, picked[9], 0.0)))
        vmax = t1[0] + t2[0]
        z = jnp.zeros((SUBLANES, LANES), F32)
        for cv, pk in zip(cands, picked):
            z = z + pk * jnp.exp(jnp.where(pk > 0, cv, vmax) - vmax)
        z = _sublane_sum(z)
        inv_z = 1.0 / z
        for v in range(n_vr):
            rows = pl.ds(v * SUBLANES, SUBLANES)
            c1 = jnp.zeros((SUBLANES, LANES), F32)
            for a in range(PEER_TOPK):
                c1 = jnp.where(r1[v] == float(a), counts[a], c1)
            a1 = jnp.exp(v1[v] - t1[0]) * inv_z
            a1_ref[h, lb, rows, :] = a1
            c1_ref[h, lb, rows, :] = c1
        for vv in range(n_vr // 2):
            rows = pl.ds(vv * 2 * SUBLANES, 2 * SUBLANES)
            e2 = jnp.concatenate([jnp.exp(v2[2 * vv] - t2[0]), jnp.exp(v2[2 * vv + 1] - t2[0])], axis=0)
            rr = jnp.concatenate([r2[2 * vv], r2[2 * vv + 1]], axis=0)
            e2_ref[h, lb, rows, :] = e2.astype(BF16)
            r2_ref[h, lb, rows, :] = rr.astype(BF16)
        return carry

    lax.fori_loop(0, PEER_HEADS * n_lb, body, 0)


def _peer_select(h2, wq_t_bf, keys_bf, tt=256):
    n, d = h2.shape
    n_lb = tt // LANES
    blk = pl.BlockSpec((PEER_HEADS, n_lb, N_KEYS, LANES), lambda i: (0, i, 0, 0))
    shape = (PEER_HEADS, n // LANES, N_KEYS, LANES)
    return pl.pallas_call(
        functools.partial(_select_kernel, n_lb=n_lb),
        grid=(n // tt,),
        in_specs=[pl.BlockSpec((tt, d), lambda i: (i, 0)),
                  pl.BlockSpec(wq_t_bf.shape, lambda i: (0, 0)),
                  pl.BlockSpec(keys_bf.shape, lambda i: (0, 0, 0))],
        out_specs=[blk, blk, blk, blk],
        out_shape=[jax.ShapeDtypeStruct(shape, F32), jax.ShapeDtypeStruct(shape, F32),
                   jax.ShapeDtypeStruct(shape, BF16), jax.ShapeDtypeStruct(shape, BF16)],
        scratch_shapes=[pltpu.VMEM((2 * PEER_HEADS, n_lb, N_KEYS, LANES), F32)],
        compiler_params=_cparams("parallel"),
        name="peer_select",
    )(h2, wq_t_bf, keys_bf)


def _gelu_exact(x):
    return 0.5 * x * (1.0 + lax.erf(x * (2.0 ** -0.5)))


def _dense_kernel(h2_ref, x1_ref, mod_ref, a1_ref, c1_ref, e2_ref, r2_ref, down_ref, up_ref,
                  g_ref, b_ref, y_ref, acc_ref, act_ref, s_ref, *, n_lb, eb):
    j = pl.program_id(1)

    @pl.when(j == 0)
    def _():
        acc_ref[...] = jnp.zeros_like(acc_ref)

    d_t = lax.dot_general(down_ref[...], h2_ref[...], NT_DIMS, preferred_element_type=F32)
    act_ref[...] = _gelu_exact(d_t).astype(BF16)
    rows_per_key = N_KEYS
    n_i1 = eb // rows_per_key
    chunk = 2 * SUBLANES
    n_chunks = rows_per_key // chunk

    for il in range(n_i1):
        i1 = j * n_i1 + il
        for lb in range(n_lb):
            lanes = slice(lb * LANES, (lb + 1) * LANES)
            accs = [None] * n_chunks
            for h in range(PEER_HEADS):
                a1 = _bcast_rows_bf16(a1_ref[h, lb, pl.ds(i1, SUBLANES, stride=0), :])
                c1 = _bcast_rows_bf16(c1_ref[h, lb, pl.ds(i1, SUBLANES, stride=0), :])
                for g in range(n_chunks):
                    rows = pl.ds(g * chunk, chunk)
                    e2 = e2_ref[h, lb, rows, :]
                    r2 = r2_ref[h, lb, rows, :]
                    term = jnp.where(r2 < c1, e2 * a1, jnp.zeros_like(e2))
                    accs[g] = term if accs[g] is None else accs[g] + term
            for g in range(n_chunks):
                rows = slice(il * rows_per_key + g * chunk, il * rows_per_key + (g + 1) * chunk)
                s_ref[rows, lanes] = accs[g] * act_ref[rows, lanes]
    acc_ref[...] += jnp.dot(up_ref[...], s_ref[...], preferred_element_type=F32)

    @pl.when(j == pl.num_programs(1) - 1)
    def _():
        peer = acc_ref[...].T
        gt2 = mod_ref[0, 5:6, :]
        y_ref[...] = _layer_norm_rows(DN_ALPHA * x1_ref[...] + (1.0 + gt2) * peer, g_ref[...], b_ref[...])


def _peer_dense(h2, x1, mod, sel, down_bf, up_t_bf, ln_g, ln_b, seq, tt=512, eb=1024):
    n, d = h2.shape
    n_exp = down_bf.shape[0]
    n_lb = tt // LANES
    tok = pl.BlockSpec((tt, d), lambda i, j: (i, 0))
    sblk = pl.BlockSpec((PEER_HEADS, n_lb, N_KEYS, LANES), lambda i, j: (0, i, 0, 0))
    vec = pl.BlockSpec((1, d), lambda i, j: (0, 0))
    return pl.pallas_call(
        functools.partial(_dense_kernel, n_lb=n_lb, eb=eb),
        grid=(n // tt, n_exp // eb),
        in_specs=[tok, tok, pl.BlockSpec((1, 6, d), lambda i, j: (i * tt // seq, 0, 0)),
                  sblk, sblk, sblk, sblk,
                  pl.BlockSpec((eb, d), lambda i, j: (j, 0)),
                  pl.BlockSpec((d, eb), lambda i, j: (0, j)),
                  vec, vec],
        out_specs=tok,
        out_shape=jax.ShapeDtypeStruct((n, d), F32),
        scratch_shapes=[pltpu.VMEM((d, tt), F32), pltpu.VMEM((eb, tt), BF16), pltpu.VMEM((eb, tt), BF16)],
        compiler_params=_cparams("parallel", "arbitrary"),
        name="peer_dense",
    )(h2, x1, mod, *sel, down_bf, up_t_bf, ln_g.reshape(1, -1), ln_b.reshape(1, -1))


def _encoder_layer(x, mod, weights):
    (w_in_bf, sink, gn_a, gn_b, w_out_bf, ln1_g, ln1_b, wq_t_bf, keys_bf, down_bf, up_t_bf,
     ln2_g, ln2_b) = weights
    b, s, d = x.shape
    tables = _rope_tables(s)
    qa, ka, va, qb, kb2, vb2 = _qkv(x, mod, w_in_bf, tables)
    branch = []
    for window, dil in DILATION_PAIRS:
        branch.append(_band_attention(qa, ka, va, dilation=dil, radius=window // (2 * dil),
                                      kv_shared=False, want_lse=True))
    (ob,) = _band_attention(qb, kb2, vb2, dilation=1, radius=B_RADIUS, kv_shared=True, sink=sink)
    (o1, l1), (o4, l4), (o16, l16) = branch
    x1, h2 = _post(o1, o4, o16, l1, l4, l16, ob, x, mod, gn_a, gn_b, w_out_bf, ln1_g, ln1_b)
    h2f = h2.reshape(b * s, d)
    sel = _peer_select(h2f, wq_t_bf, keys_bf)
    y = _peer_dense(h2f, x1.reshape(b * s, d), mod, sel, down_bf, up_t_bf, ln2_g, ln2_b, s)
    return y.reshape(b, s, d)


def kernel(x_prompt, x_sample, c_prompt, c_sample, w_mod, b_mod, w_in, attn_sink, gn_a, gn_b, w_out,
           ln1_g, ln1_b, w_query, sub_keys, expert_down, expert_up, ln2_g, ln2_b):
    l = 0
    d = x_prompt.shape[-1]
    nb_p = c_prompt.shape[0]
    c_all = jnp.concatenate([c_prompt, c_sample], axis=0)
    mod = _modulation(c_all, w_mod[l], b_mod[l]).reshape(c_all.shape[0], 6, d)
    col = np.ones((IN_WIDTH,), np.float32)
    col[:A_WIDTH] = HEAD_DIM ** -0.5
    col[3 * A_WIDTH:3 * A_WIDTH + B_WIDTH] = HEAD_DIM ** -0.5
    weights = (
        (w_in[l] * col[None, :]).astype(BF16), attn_sink[l], gn_a[l], gn_b[l], w_out[l].astype(BF16),
        ln1_g[l], ln1_b[l], w_query[l].T.astype(BF16),
        sub_keys[l].reshape(2 * PEER_HEADS, N_KEYS, -1).astype(BF16),
        expert_down[l].astype(BF16), expert_up[l].T.astype(BF16), ln2_g[l], ln2_b[l])
    y_prompt = _encoder_layer(x_prompt, mod[:nb_p], weights)
    y_sample = _encoder_layer(x_sample, mod[nb_p:], weights)
    return (y_prompt, y_sample)
```

```python
import functools

import jax
import jax.numpy as jnp
import numpy as np
from jax import lax
from jax.experimental import pallas as pl
from jax.experimental.pallas import tpu as pltpu

F32 = jnp.float32
BF16 = jnp.bfloat16

LANES = 128
SUBLANES = 8
VMEM_LIMIT = 56 * 1024 * 1024

HEAD_DIM = 64
A_HEADS = 8
B_Q_HEADS = 8
B_KV_HEADS = 2
A_WIDTH = A_HEADS * HEAD_DIM
B_WIDTH = B_Q_HEADS * HEAD_DIM
KV_B_WIDTH = B_KV_HEADS * HEAD_DIM
IN_WIDTH = 3 * A_WIDTH + B_WIDTH + 2 * KV_B_WIDTH
DILATION_PAIRS = ((128, 1), (512, 4), (2048, 16))
B_RADIUS = 128
ROT_DIM = HEAD_DIM // 4
ROPE_THETA = 500000.0
N_KEYS = 128
PEER_HEADS = 8
PEER_TOPK = 16
LN_EPS = 1e-5
RMS_EPS = 1e-6
DEPTH = 1
DN_ALPHA = (2 * DEPTH) ** 0.25

NT_DIMS = (((1,), (1,)), ((), ()))


def _cparams(*sem):
    return pltpu.CompilerParams(dimension_semantics=sem, vmem_limit_bytes=VMEM_LIMIT)


def _mod_kernel(c_ref, w_ref, b_ref, o_ref):
    c = c_ref[...]
    a = c * jax.nn.sigmoid(c)
    o_ref[...] = jnp.dot(a, w_ref[...], preferred_element_type=F32,
                         precision=lax.Precision.HIGHEST) + b_ref[...]


def _modulation(c, w_mod, b_mod):
    n, d = c.shape
    width = w_mod.shape[1]
    tn = 1536
    return pl.pallas_call(
        _mod_kernel,
        grid=(width // tn,),
        in_specs=[pl.BlockSpec((n, d), lambda j: (0, 0)),
                  pl.BlockSpec((d, tn), lambda j: (0, j)),
                  pl.BlockSpec((1, tn), lambda j: (0, j))],
        out_specs=pl.BlockSpec((n, tn), lambda j: (0, j)),
        out_shape=jax.ShapeDtypeStruct((n, width), F32),
        compiler_params=_cparams("arbitrary"),
        name="mod_kernel",
    )(c, w_mod, b_mod.reshape(1, width))


def _rope_tables(seq):
    half = ROT_DIM // 2
    inv = ROPE_THETA ** (-jnp.arange(half, dtype=F32) * 2.0 / ROT_DIM)
    ang = jnp.arange(seq, dtype=F32)[:, None] * inv[None, :]
    cos, sin = jnp.cos(ang), jnp.sin(ang)
    ones = jnp.ones((seq, HEAD_DIM - ROT_DIM), F32)
    zeros = jnp.zeros((seq, HEAD_DIM - ROT_DIM), F32)
    zh = jnp.zeros((seq, half), F32)
    c_head = jnp.concatenate([cos, cos, ones], axis=-1)
    s1_head = jnp.concatenate([zh, sin, zeros], axis=-1)
    s2_head = jnp.concatenate([-sin, zh, zeros], axis=-1)
    two = lambda t: jnp.concatenate([t, t], axis=-1)
    return two(c_head), two(s1_head), two(s2_head)


def _qkv_kernel(x_ref, mod_ref, w_ref, c_ref, s1_ref, s2_ref,
                qa_ref, ka_ref, va_ref, qb_ref, kb_ref, vb_ref):
    x = x_ref[0]
    sh1 = mod_ref[0, 0:1, :]
    sc1 = mod_ref[0, 1:2, :]
    h = (x * (1.0 + sc1) + sh1).astype(BF16)
    proj = jnp.dot(h, w_ref[...], preferred_element_type=F32)
    cs, s1, s2 = c_ref[...], s1_ref[...], s2_ref[...]
    half = ROT_DIM // 2

    def rope(col):
        t = proj[:, col:col + LANES]
        return t * cs + pltpu.roll(t, half, 1) * s1 + pltpu.roll(t, LANES - half, 1) * s2

    for j in range(A_WIDTH // LANES):
        sl = slice(j * LANES, (j + 1) * LANES)
        qa_ref[0, :, sl] = rope(j * LANES).astype(BF16)
        ka_ref[0, :, sl] = rope(A_WIDTH + j * LANES).astype(BF16)
        va_ref[0, :, sl] = proj[:, 2 * A_WIDTH + j * LANES:2 * A_WIDTH + (j + 1) * LANES].astype(BF16)
        qb_ref[0, :, sl] = rope(3 * A_WIDTH + j * LANES).astype(BF16)
    kb = rope(3 * A_WIDTH + B_WIDTH)
    vb = proj[:, 3 * A_WIDTH + B_WIDTH + KV_B_WIDTH:]
    lane = lax.broadcasted_iota(jnp.int32, (1, LANES), 1)
    low = lane < HEAD_DIM
    for src, dst in ((kb, kb_ref), (vb, vb_ref)):
        sw = pltpu.roll(src, HEAD_DIM, 1)
        dst[0, :, 0:LANES] = jnp.where(low, src, sw).astype(BF16)
        dst[0, :, LANES:2 * LANES] = jnp.where(low, sw, src).astype(BF16)


def _qkv(x, mod, w_in_bf, tables, tm=512):
    b, s, d = x.shape
    full = lambda shape: pl.BlockSpec(shape, lambda i, j: (0,) * len(shape))
    tok = lambda w: pl.BlockSpec((1, tm, w), lambda i, j: (i, j, 0))
    tab = pl.BlockSpec((tm, LANES), lambda i, j: (j, 0))
    outs = [jax.ShapeDtypeStruct((b, s, w), BF16)
            for w in (A_WIDTH, A_WIDTH, A_WIDTH, B_WIDTH, 2 * KV_B_WIDTH, 2 * KV_B_WIDTH)]
    return pl.pallas_call(
        _qkv_kernel,
        grid=(b, s // tm),
        in_specs=[tok(d), pl.BlockSpec((1, 6, d), lambda i, j: (i, 0, 0)),
                  full((d, IN_WIDTH)), tab, tab, tab],
        out_specs=[tok(o.shape[-1]) for o in outs],
        out_shape=outs,
        compiler_params=_cparams("parallel", "arbitrary"),
        name="qkv_kernel",
    )(x, mod, w_in_bf, *tables)


def _band_kernel(*refs, length, radius, tq, tk, has_sink, want_lse):
    if has_sink:
        sink_ref, q_ref, k_ref, v_ref = refs[:4]
        outs = refs[4:]
    else:
        q_ref, k_ref, v_ref = refs[:3]
        outs = refs[3:]
    o_ref = outs[0]
    lse_ref = outs[1] if want_lse else None
    hp = pl.program_id(2)
    lane = lax.broadcasted_iota(jnp.int32, (1, LANES), 1)
    low = lane < HEAD_DIM
    qi = lax.broadcasted_iota(jnp.int32, (tq, 1), 0)
    ki = lax.broadcasted_iota(jnp.int32, (1, tk), 1)

    def body(i, carry):
        q0 = pl.multiple_of(i * tq, tq)
        start = pl.multiple_of(jnp.clip(q0 - radius, 0, length - tk), 64)
        q = q_ref[0, pl.ds(q0, tq), :]
        k = k_ref[0, pl.ds(start, tk), :]
        v = v_ref[0, pl.ds(start, tk), :]
        valid = jnp.abs((start + ki) - (q0 + qi)) <= radius
        o_heads, lse_heads = [], []
        for hh in range(2):
            sel = low if hh == 0 else jnp.logical_not(low)
            qh = jnp.where(sel, q, jnp.zeros_like(q))
            s = lax.dot_general(qh, k, NT_DIMS, preferred_element_type=F32)
            s = jnp.where(valid, s, -jnp.inf)
            m = jnp.max(s, axis=-1, keepdims=True)
            if has_sink:
                sk = sink_ref[hp * 2 + hh]
                m = jnp.maximum(m, sk)
            p = jnp.exp(s - m)
            den = jnp.sum(p, axis=-1, keepdims=True)
            if has_sink:
                den = den + jnp.exp(sk - m)
            o = jnp.dot(p.astype(BF16), v, preferred_element_type=F32)
            o_heads.append(o / den)
            lse_heads.append(m + jnp.log(den))
        o_ref[0, pl.ds(q0, tq), :] = jnp.where(low, o_heads[0], o_heads[1]).astype(o_ref.dtype)
        if want_lse:
            lse_ref[0, pl.ds(q0, tq), :] = jnp.where(low, lse_heads[0], lse_heads[1])
        return carry

    n_blocks = length // tq
    lax.fori_loop(0, n_blocks, body, 0, unroll=min(4, n_blocks))


def _band_attention(q, k, v, *, dilation, radius, kv_shared, sink=None, want_lse=False):
    b, s, width = q.shape
    d = dilation
    length = s // d
    n_hp = width // LANES
    tq = min(128, length)
    tk = min(tq + 2 * radius, length)
    qv = q.reshape(b, length, d * width)
    kw = k.shape[-1]
    kv_blocks = kw // LANES
    kview = k.reshape(b, length, d * kw)
    vview = v.reshape(b, length, d * kw)
    q_spec = pl.BlockSpec((1, length, LANES), lambda i, r, h: (i, 0, r * n_hp + h))
    if kv_shared:
        k_spec = pl.BlockSpec((1, length, LANES), lambda i, r, h: (i, 0, r * kv_blocks + h // 2))
    else:
        k_spec = q_spec
    in_specs = [q_spec, k_spec, k_spec]
    args = [qv, kview, vview]
    if sink is not None:
        in_specs = [pl.BlockSpec(memory_space=pltpu.SMEM)] + in_specs
        args = [sink] + args
    out_shape = [jax.ShapeDtypeStruct(qv.shape, BF16)]
    out_specs = [q_spec]
    if want_lse:
        out_shape.append(jax.ShapeDtypeStruct(qv.shape, F32))
        out_specs.append(q_spec)
    kern = functools.partial(_band_kernel, length=length, radius=radius, tq=tq, tk=tk,
                             has_sink=sink is not None, want_lse=want_lse)
    res = pl.pallas_call(
        kern,
        grid=(b, d, n_hp),
        in_specs=in_specs,
        out_specs=out_specs,
        out_shape=out_shape,
        compiler_params=_cparams("parallel", "arbitrary", "arbitrary"),
        name=f"band_attn_d{d}_r{radius}",
    )(*args)
    return [t.reshape(b, s, width) for t in res]


def _layer_norm_rows(y, g, bias):
    mu = jnp.mean(y, axis=-1, keepdims=True)
    yc = y - mu
    var = jnp.mean(yc * yc, axis=-1, keepdims=True)
    return yc * lax.rsqrt(var + LN_EPS) * g + bias


def _post_kernel(o1_ref, o4_ref, o16_ref, l1_ref, l4_ref, l16_ref, ob_ref, x_ref, mod_ref,
                 gna_ref, gnb_ref, w_ref, g_ref, b_ref, x1_ref, h2_ref):
    l1, l4, l16 = l1_ref[0], l4_ref[0], l16_ref[0]
    mx = jnp.maximum(jnp.maximum(l1, l4), l16)
    e1, e4, e16 = jnp.exp(l1 - mx), jnp.exp(l4 - mx), jnp.exp(l16 - mx)
    num = (e1 * o1_ref[0].astype(F32) + e4 * o4_ref[0].astype(F32) + e16 * o16_ref[0].astype(F32))
    oa = num / (e1 + e4 + e16)
    ob = ob_ref[0].astype(F32)

    def rms(t, g):
        return t * lax.rsqrt(jnp.mean(t * t, axis=-1, keepdims=True) + RMS_EPS) * g

    merged = jnp.concatenate([rms(oa, gna_ref[...]), rms(ob, gnb_ref[...])], axis=-1).astype(BF16)
    mix = jnp.dot(merged, w_ref[...], preferred_element_type=F32)
    gt1 = mod_ref[0, 2:3, :]
    sh2 = mod_ref[0, 3:4, :]
    sc2 = mod_ref[0, 4:5, :]
    x1 = _layer_norm_rows(DN_ALPHA * x_ref[0] + (1.0 + gt1) * mix, g_ref[...], b_ref[...])
    x1_ref[0] = x1
    h2_ref[0] = (x1 * (1.0 + sc2) + sh2).astype(BF16)


def _post(o1, o4, o16, l1, l4, l16, ob, x, mod, gn_a, gn_b, w_out_bf, ln_g, ln_b, tm=256):
    b, s, d = x.shape
    tok = lambda w: pl.BlockSpec((1, tm, w), lambda i, j: (i, j, 0))
    full = lambda shape: pl.BlockSpec(shape, lambda i, j: (0,) * len(shape))
    return pl.pallas_call(
        _post_kernel,
        grid=(b, s // tm),
        in_specs=[tok(A_WIDTH)] * 3 + [tok(A_WIDTH)] * 3 + [tok(B_WIDTH), tok(d),
                  pl.BlockSpec((1, 6, d), lambda i, j: (i, 0, 0)),
                  full((1, A_WIDTH)), full((1, B_WIDTH)), full((d, d)), full((1, d)), full((1, d))],
        out_specs=[tok(d), tok(d)],
        out_shape=[jax.ShapeDtypeStruct((b, s, d), F32), jax.ShapeDtypeStruct((b, s, d), BF16)],
        compiler_params=_cparams("parallel", "arbitrary"),
        name="post_kernel",
    )(o1, o4, o16, l1, l4, l16, ob, x, mod, gn_a.reshape(1, -1), gn_b.reshape(1, -1),
      w_out_bf, ln_g.reshape(1, -1), ln_b.reshape(1, -1))


NEG = -jnp.inf


def _lex_allmax(val, idx):
    for shift in (4, 2, 1):
        v2 = pltpu.roll(val, shift, 0)
        i2 = pltpu.roll(idx, shift, 0)
        better = (v2 > val) | ((v2 == val) & (i2 < idx))
        val = jnp.where(better, v2, val)
        idx = jnp.where(better, i2, idx)
    return val, idx


def _extract_top(vals, ids, n_top):
    ranks = [jnp.full(v.shape, float(n_top), F32) for v in vals]
    tops = []
    for kk in range(n_top):
        bv, bi = vals[0], ids[0]
        for v, i in zip(vals[1:], ids[1:]):
            gt = v > bv
            bv = jnp.where(gt, v, bv)
            bi = jnp.where(gt, i, bi)
        bv, bi = _lex_allmax(bv, bi)
        tops.append(bv)
        new_vals, new_ranks = [], []
        for v, i, r in zip(vals, ids, ranks):
            hit = i == bi
            new_vals.append(jnp.where(hit, NEG, v))
            new_ranks.append(jnp.where(hit, float(kk), r))
        vals, ranks = new_vals, new_ranks
    return ranks, tops


def _sublane_sum(x):
    for shift in (4, 2, 1):
        x = x + pltpu.roll(x, shift, 0)
    return x


def _bcast_rows_bf16(x):
    return jnp.concatenate([x, x], axis=0).astype(BF16)


def _select_kernel(h2_ref, wq_ref, keys_ref, a1_ref, c1_ref, e2_ref, r2_ref, s_scr, *, n_lb):
    q_all = lax.dot_general(wq_ref[...], h2_ref[...], NT_DIMS,
                            preferred_element_type=F32).astype(BF16)
    for hp in range(2 * PEER_HEADS):
        q_t = q_all[hp * LANES:(hp + 1) * LANES, :]
        s_t = jnp.dot(keys_ref[hp], q_t, preferred_element_type=F32)
        for lb in range(n_lb):
            s_scr[hp, lb] = s_t[:, lb * LANES:(lb + 1) * LANES]

    n_vr = N_KEYS // SUBLANES
    sub = lax.broadcasted_iota(jnp.int32, (SUBLANES, LANES), 0)
    key_ids = [sub + v * SUBLANES for v in range(n_vr)]
    subf = sub.astype(F32)
    stair_limit = [None, None] + [PEER_TOPK // (a + 1) for a in range(1, 8)] + [None]
    cand_ids = ([sub, sub + SUBLANES] + [sub + a * PEER_TOPK for a in range(1, 8)]
                + [(sub + SUBLANES) * PEER_TOPK])

    def body(t, carry):
        h = t // n_lb
        lb = t % n_lb
        halves = []
        for p in range(2):
            s_full = s_scr[2 * h + p, lb]
            vals = [s_full[v * SUBLANES:(v + 1) * SUBLANES, :] for v in range(n_vr)]
            ranks, tops = _extract_top(vals, key_ids, PEER_TOPK)
            halves.append((vals, ranks, tops))
        (v1, r1, t1), (v2, r2, t2) = halves
        def by_sublane(tops, base):
            out = jnp.zeros((SUBLANES, LANES), F32)
            for j in range(SUBLANES):
                out = jnp.where(sub == j, tops[base + j], out)
            return out
        t2_lo, t2_hi = by_sublane(t2, 0), by_sublane(t2, SUBLANES)
        t1_hi = by_sublane(t1, SUBLANES)
        cands = [t1[0] + t2_lo, t1[0] + t2_hi]
        for a in range(1, 8):
            cands.append(jnp.where(sub < stair_limit[a + 1], t1[a] + t2_lo, NEG))
        cands.append(t1_hi + t2[0])
        cranks, _ = _extract_top(cands, cand_ids, PEER_TOPK)
        picked = [(r < float(PEER_TOPK)).astype(F32) for r in cranks]
        counts = [_sublane_sum(picked[0] + picked[1])]
        for a in range(1, 8):
            counts.append(_sublane_sum(picked[a + 1]))
        for a in range(8, PEER_TOPK):
            counts.append(_sublane_sum(jnp.where(sub == a - 8, picked[9], 0.0)))
        vmax = t1[0] + t2[0]
        z = jnp.zeros((SUBLANES, LANES), F32)
        for cv, pk in zip(cands, picked):
            z = z + pk * jnp.exp(jnp.where(pk > 0, cv, vmax) - vmax)
        z = _sublane_sum(z)
        inv_z = 1.0 / z
        for v in range(n_vr):
            rows = pl.ds(v * SUBLANES, SUBLANES)
            c1 = jnp.zeros((SUBLANES, LANES), F32)
            for a in range(PEER_TOPK):
                c1 = jnp.where(r1[v] == float(a), counts[a], c1)
            a1 = jnp.exp(v1[v] - t1[0]) * inv_z
            a1_ref[h, lb, rows, :] = a1
            c1_ref[h, lb, rows, :] = c1
        for vv in range(n_vr // 2):
            rows = pl.ds(vv * 2 * SUBLANES, 2 * SUBLANES)
            e2 = jnp.concatenate([jnp.exp(v2[2 * vv] - t2[0]), jnp.exp(v2[2 * vv + 1] - t2[0])], axis=0)
            rr = jnp.concatenate([r2[2 * vv], r2[2 * vv + 1]], axis=0)
            e2_ref[h, lb, rows, :] = e2.astype(BF16)
            r2_ref[h, lb, rows, :] = rr.astype(BF16)
        return carry

    lax.fori_loop(0, PEER_HEADS * n_lb, body, 0)


def _peer_select(h2, wq_t_bf, keys_bf, tt=256):
    n, d = h2.shape
    n_lb = tt // LANES
    blk = pl.BlockSpec((PEER_HEADS, n_lb, N_KEYS, LANES), lambda i: (0, i, 0, 0))
    shape = (PEER_HEADS, n // LANES, N_KEYS, LANES)
    return pl.pallas_call(
        functools.partial(_select_kernel, n_lb=n_lb),
        grid=(n // tt,),
        in_specs=[pl.BlockSpec((tt, d), lambda i: (i, 0)),
                  pl.BlockSpec(wq_t_bf.shape, lambda i: (0, 0)),
                  pl.BlockSpec(keys_bf.shape, lambda i: (0, 0, 0))],
        out_specs=[blk, blk, blk, blk],
        out_shape=[jax.ShapeDtypeStruct(shape, F32), jax.ShapeDtypeStruct(shape, F32),
                   jax.ShapeDtypeStruct(shape, BF16), jax.ShapeDtypeStruct(shape, BF16)],
        scratch_shapes=[pltpu.VMEM((2 * PEER_HEADS, n_lb, N_KEYS, LANES), F32)],
        compiler_params=_cparams("parallel"),
        name="peer_select",
    )(h2, wq_t_bf, keys_bf)


def _gelu_exact(x):
    return 0.5 * x * (1.0 + lax.erf(x * (2.0 ** -0.5)))


def _dense_kernel(h2_ref, x1_ref, mod_ref, a1_ref, c1_ref, e2_ref, r2_ref, down_ref, up_ref,
                  g_ref, b_ref, y_ref, acc_ref, act_ref, s_ref, *, n_lb, eb):
    j = pl.program_id(1)

    @pl.when(j == 0)
    def _():
        acc_ref[...] = jnp.zeros_like(acc_ref)

    d_t = lax.dot_general(down_ref[...], h2_ref[...], NT_DIMS, preferred_element_type=F32)
    act_ref[...] = _gelu_exact(d_t).astype(BF16)
    rows_per_key = N_KEYS
    n_i1 = eb // rows_per_key
    chunk = 2 * SUBLANES
    n_chunks = rows_per_key // chunk

    for il in range(n_i1):
        i1 = j * n_i1 + il
        for lb in range(n_lb):
            lanes = slice(lb * LANES, (lb + 1) * LANES)
            accs = [None] * n_chunks
            for h in range(PEER_HEADS):
                a1 = _bcast_rows_bf16(a1_ref[h, lb, pl.ds(i1, SUBLANES, stride=0), :])
                c1 = _bcast_rows_bf16(c1_ref[h, lb, pl.ds(i1, SUBLANES, stride=0), :])
                for g in range(n_chunks):
                    rows = pl.ds(g * chunk, chunk)
                    e2 = e2_ref[h, lb, rows, :]
                    r2 = r2_ref[h, lb, rows, :]
                    term = jnp.where(r2 < c1, e2 * a1, jnp.zeros_like(e2))
                    accs[g] = term if accs[g] is None else accs[g] + term
            for g in range(n_chunks):
                rows = slice(il * rows_per_key + g * chunk, il * rows_per_key + (g + 1) * chunk)
                s_ref[rows, lanes] = accs[g] * act_ref[rows, lanes]
    acc_ref[...] += jnp.dot(up_ref[...], s_ref[...], preferred_element_type=F32)

    @pl.when(j == pl.num_programs(1) - 1)
    def _():
        peer = acc_ref[...].T
        gt2 = mod_ref[0, 5:6, :]
        y_ref[...] = _layer_norm_rows(DN_ALPHA * x1_ref[...] + (1.0 + gt2) * peer, g_ref[...], b_ref[...])


def _peer_dense(h2, x1, mod, sel, down_bf, up_t_bf, ln_g, ln_b, seq, tt=512, eb=1024):
    n, d = h2.shape
    n_exp = down_bf.shape[0]
    n_lb = tt // LANES
    tok = pl.BlockSpec((tt, d), lambda i, j: (i, 0))
    sblk = pl.BlockSpec((PEER_HEADS, n_lb, N_KEYS, LANES), lambda i, j: (0, i, 0, 0))
    vec = pl.BlockSpec((1, d), lambda i, j: (0, 0))
    return pl.pallas_call(
        functools.partial(_dense_kernel, n_lb=n_lb, eb=eb),
        grid=(n // tt, n_exp // eb),
        in_specs=[tok, tok, pl.BlockSpec((1, 6, d), lambda i, j: (i * tt // seq, 0, 0)),
                  sblk, sblk, sblk, sblk,
                  pl.BlockSpec((eb, d), lambda i, j: (j, 0)),
                  pl.BlockSpec((d, eb), lambda i, j: (0, j)),
                  vec, vec],
        out_specs=tok,
        out_shape=jax.ShapeDtypeStruct((n, d), F32),
        scratch_shapes=[pltpu.VMEM((d, tt), F32), pltpu.VMEM((eb, tt), BF16), pltpu.VMEM((eb, tt), BF16)],
        compiler_params=_cparams("parallel", "arbitrary"),
        name="peer_dense",
    )(h2, x1, mod, *sel, down_bf, up_t_bf, ln_g.reshape(1, -1), ln_b.reshape(1, -1))


def _encoder_layer(x, mod, weights):
    (w_in_bf, sink, gn_a, gn_b, w_out_bf, ln1_g, ln1_b, wq_t_bf, keys_bf, down_bf, up_t_bf,
     ln2_g, ln2_b) = weights
    b, s, d = x.shape
    tables = _rope_tables(s)
    qa, ka, va, qb, kb2, vb2 = _qkv(x, mod, w_in_bf, tables)
    branch = []
    for window, dil in DILATION_PAIRS:
        branch.append(_band_attention(qa, ka, va, dilation=dil, radius=window // (2 * dil),
                                      kv_shared=False, want_lse=True))
    (ob,) = _band_attention(qb, kb2, vb2, dilation=1, radius=B_RADIUS, kv_shared=True, sink=sink)
    (o1, l1), (o4, l4), (o16, l16) = branch
    x1, h2 = _post(o1, o4, o16, l1, l4, l16, ob, x, mod, gn_a, gn_b, w_out_bf, ln1_g, ln1_b)
    h2f = h2.reshape(b * s, d)
    sel = _peer_select(h2f, wq_t_bf, keys_bf)
    y = _peer_dense(h2f, x1.reshape(b * s, d), mod, sel, down_bf, up_t_bf, ln2_g, ln2_b, s)
    return y.reshape(b, s, d)


def kernel(x_prompt, x_sample, c_prompt, c_sample, w_mod, b_mod, w_in, attn_sink, gn_a, gn_b, w_out,
           ln1_g, ln1_b, w_query, sub_keys, expert_down, expert_up, ln2_g, ln2_b):
    l = 0
    d = x_prompt.shape[-1]
    nb_p = c_prompt.shape[0]
    c_all = jnp.concatenate([c_prompt, c_sample], axis=0)
    mod = _modulation(c_all, w_mod[l], b_mod[l]).reshape(c_all.shape[0], 6, d)
    col = np.ones((IN_WIDTH,), np.float32)
    col[:A_WIDTH] = HEAD_DIM ** -0.5
    col[3 * A_WIDTH:3 * A_WIDTH + B_WIDTH] = HEAD_DIM ** -0.5
    weights = (
        (w_in[l] * col[None, :]).astype(BF16), attn_sink[l], gn_a[l], gn_b[l], w_out[l].astype(BF16),
        ln1_g[l], ln1_b[l], w_query[l].T.astype(BF16),
        sub_keys[l].reshape(2 * PEER_HEADS, N_KEYS, -1).astype(BF16),
        expert_down[l].astype(BF16), expert_up[l].T.astype(BF16), ln2_g[l], ln2_b[l])
    y_prompt = _encoder_layer(x_prompt, mod[:nb_p], weights)
    y_sample = _encoder_layer(x_sample, mod[nb_p:], weights)
    return (y_prompt, y_sample)
```
